```python
import math, functools
import jax, jax.numpy as jnp
from jax import lax
import numpy as np

D_MODEL = 1024
BATCH = 8
SEQ = 2048
DEPTH = 2
DEC_BATCH = 32
DEC_SEQ = 1
PAST_LEN = 16384
PAGE_SIZE = 128

HA = 4
DK_A = 64
DV_A = 2 * DK_A
HB = 4
DH_B = 64
HC = 4
DH_C = 64
CHUNK = 128
D_MIX = HA * DV_A + HB * DH_B + HC * DH_C
ROPE_THETA = 500000.0
ROT_DIM = DK_A // 4
QBLK = 128
N_EXPERTS = 32
TOP_K = 4
D_FF = D_MODEL
SWIGLU_LIMIT = 7.0
SWIGLU_ALPHA = 1.702
MOE_BLK = 128
D_PLE = 256
EPS = 1e-6
IN_SIZES = (HA * 2 * DK_A, HA * 2 * DK_A, HA * DV_A, HB * DH_B, HB * DH_B, HB * DH_B, HB, HC * DH_C, HC * DH_C)
N_IN = sum(IN_SIZES)

kernel_name = 'hybrid_diffattn_fox_chunkmlp_moe_step'


def rms_norm(x, g):
    xf = x.astype(jnp.float32)
    y = xf * lax.rsqrt(jnp.mean(xf * xf, axis=-1, keepdims=True) + EPS)
    return (y * g.astype(jnp.float32)).astype(x.dtype)


def rope(x, pos):
    half = ROT_DIM // 2
    inv = ROPE_THETA ** (-2.0 * jnp.arange(half, dtype=jnp.float32) / ROT_DIM)
    ang = pos.astype(jnp.float32)[:, None] * inv[None, :]
    shape = (ang.shape[0],) + (1,) * (x.ndim - 3) + (half,)
    cos = jnp.cos(ang).reshape(shape)
    sin = jnp.sin(ang).reshape(shape)
    xf = x.astype(jnp.float32)
    x1, x2, rest = xf[..., :half], xf[..., half:ROT_DIM], xf[..., ROT_DIM:]
    return jnp.concatenate([x1 * cos - x2 * sin, x1 * sin + x2 * cos, rest], axis=-1).astype(x.dtype)


def project(n1, pos, w_in, b_f, gq_a, gk_a, gq_b, gk_b, g_sgu):
    bsz, t = n1.shape[:2]
    cuts = np.cumsum(IN_SIZES)[:-1].tolist()
    qa, ka, va, qb, kb, vb, fl, uc, vc = jnp.split(n1 @ w_in, cuts, axis=-1)
    qa = rope(rms_norm(qa.reshape(bsz, t, HA, 2, DK_A), gq_a), pos)
    ka = rope(rms_norm(ka.reshape(bsz, t, HA, 2, DK_A), gk_a), pos)
    va = va.reshape(bsz, t, HA, DV_A)
    qb = rms_norm(qb.reshape(bsz, t, HB, DH_B), gq_b)
    kb = rms_norm(kb.reshape(bsz, t, HB, DH_B), gk_b)
    vb = vb.reshape(bsz, t, HB, DH_B)
    logf = jax.nn.log_sigmoid(fl.astype(jnp.float32) + b_f.astype(jnp.float32))
    uc = uc.reshape(bsz, t, HC, DH_C)
    vc = rms_norm(vc.reshape(bsz, t, HC, DH_C), g_sgu)
    return qa, ka, va, qb, kb, vb, logf, uc, vc


def diff_attn_prompt(q, k, v, lam):
    bsz, s = q.shape[:2]
    nb = s // QBLK
    scale = DK_A ** -0.5
    qb = q.reshape(bsz, nb, QBLK, HA, 2, DK_A).swapaxes(0, 1)
    kpos = jnp.arange(s)

    def block(args):
        qi, i = args
        qpos = i * QBLK + jnp.arange(QBLK)
        sc = jnp.einsum('bqhcd,bkhcd->bhcqk', qi, k).astype(jnp.float32) * scale
        sc = jnp.where(kpos[None, :] <= qpos[:, None], sc, -jnp.inf)
        pr = jax.nn.softmax(sc, axis=-1)
        pd = (pr[:, :, 0] - lam * pr[:, :, 1]).astype(v.dtype)
        return jnp.einsum('bhqk,bkhd->bqhd', pd, v)

    o = lax.map(block, (qb, jnp.arange(nb)))
    return o.swapaxes(0, 1).reshape(bsz, s, HA, DV_A)


def diff_attn_sample(q, k_new, v_new, lam, k_past, v_past):
    t = q.shape[1]
    p = k_past.shape[1]
    scale = DK_A ** -0.5
    causal = jnp.arange(t)[None, :] <= jnp.arange(t)[:, None]
    s_past = jnp.einsum('bqhcd,bkhcd->bhcqk', q, k_past).astype(jnp.float32)
    s_new = jnp.where(causal, jnp.einsum('bqhcd,bkhcd->bhcqk', q, k_new).astype(jnp.float32), -jnp.inf)
    pr = jax.nn.softmax(jnp.concatenate([s_past, s_new], axis=-1) * scale, axis=-1)
    pd = (pr[:, :, 0] - lam * pr[:, :, 1]).astype(v_new.dtype)
    return jnp.einsum('bhqk,bkhd->bqhd', pd[..., :p], v_past) + jnp.einsum('bhqk,bkhd->bqhd', pd[..., p:], v_new)


def fox_prompt(q, k, v, logf):
    bsz, s = q.shape[:2]
    nb = s // QBLK
    scale = DH_B ** -0.5
    c = jnp.cumsum(logf, axis=1).transpose(0, 2, 1)
    qb = q.reshape(bsz, nb, QBLK, HB, DH_B).swapaxes(0, 1)
    cb = c.reshape(bsz, HB, nb, QBLK).transpose(2, 0, 1, 3)
    kpos = jnp.arange(s)

    def block(args):
        qi, ci, i = args
        qpos = i * QBLK + jnp.arange(QBLK)
        sc = jnp.einsum('bqhd,bkhd->bhqk', qi, k).astype(jnp.float32) * scale + ci[..., None] - c[:, :, None, :]
        sc = jnp.where(kpos[None, :] <= qpos[:, None], sc, -jnp.inf)
        pr = jax.nn.softmax(sc, axis=-1).astype(v.dtype)
        return jnp.einsum('bhqk,bkhd->bqhd', pr, v)

    o = lax.map(block, (qb, cb, jnp.arange(nb)))
    return o.swapaxes(0, 1).reshape(bsz, s, HB, DH_B)


def fox_sample(q, k_new, v_new, logf_new, k_past, v_past, logf_past):
    t = q.shape[1]
    p = k_past.shape[1]
    scale = DH_B ** -0.5
    c = jnp.cumsum(jnp.concatenate([logf_past.astype(jnp.float32), logf_new], axis=1), axis=1).transpose(0, 2, 1)
    cq = c[:, :, p:]
    causal = jnp.arange(t)[None, :] <= jnp.arange(t)[:, None]
    s_past = jnp.einsum('bqhd,bkhd->bhqk', q, k_past).astype(jnp.float32)
    s_new = jnp.where(causal, jnp.einsum('bqhd,bkhd->bhqk', q, k_new).astype(jnp.float32), -jnp.inf)
    sc = jnp.concatenate([s_past, s_new], axis=-1) * scale + cq[..., None] - c[:, :, None, :]
    pr = jax.nn.softmax(sc, axis=-1).astype(v_new.dtype)
    return jnp.einsum('bhqk,bkhd->bqhd', pr[..., :p], v_past) + jnp.einsum('bhqk,bkhd->bqhd', pr[..., p:], v_new)


def chunk_mlp(u, v, w_s, b_s):
    bsz, t = u.shape[:2]
    lc = min(t, CHUNK)
    nc = t // lc
    w = jnp.where(jnp.tril(jnp.ones((lc, lc), dtype=bool)), w_s[:, :lc, :lc], 0)
    vb = v.reshape(bsz, nc, lc, HC, DH_C)
    sg = jnp.einsum('gts,bcsgd->bctgd', w, vb) + b_s[:, :lc].T[:, :, None]
    return u * sg.reshape(bsz, t, HC, DH_C)


def moe(x, w_router, b_router, w1, b1, w2, b2):
    n_tok, d = x.shape
    logits = (x @ w_router + b_router).astype(jnp.float32)
    top_v, top_e = lax.top_k(logits, TOP_K)
    gates = jax.nn.softmax(top_v, axis=-1).astype(x.dtype)
    n_as = n_tok * TOP_K
    flat_e = top_e.reshape(n_as)
    flat_tok = jnp.arange(n_as, dtype=jnp.int32) // TOP_K
    order = jnp.argsort(flat_e)
    se, stok, sg = flat_e[order], flat_tok[order], gates.reshape(n_as)[order]
    counts = jnp.bincount(flat_e, length=N_EXPERTS)
    padded = (counts + MOE_BLK - 1) // MOE_BLK * MOE_BLK
    pend = jnp.cumsum(padded)
    pstart = pend - padded
    start = jnp.cumsum(counts) - counts
    dest = pstart[se] + jnp.arange(n_as, dtype=jnp.int32) - start[se]
    n_blocks = -(-n_as // MOE_BLK) + N_EXPERTS
    n_slots = n_blocks * MOE_BLK
    slot_tok = jnp.full((n_slots,), n_tok, jnp.int32).at[dest].set(stok)
    slot_gate = jnp.zeros((n_slots,), x.dtype).at[dest].set(sg)
    blk_e = jnp.minimum(jnp.searchsorted(pend, jnp.arange(n_blocks, dtype=jnp.int32) * MOE_BLK, side='right'), N_EXPERTS - 1)
    xp = jnp.concatenate([x, jnp.zeros((1, d), x.dtype)], axis=0)
    xb = xp[slot_tok].reshape(n_blocks, MOE_BLK, d)

    def expert_block(args):
        xe, e = args
        hmid = xe @ w1[e] + b1[e]
        g, up = hmid[:, :D_FF], hmid[:, D_FF:]
        g = jnp.minimum(g, SWIGLU_LIMIT)
        up = jnp.clip(up, -SWIGLU_LIMIT, SWIGLU_LIMIT)
        act = g * jax.nn.sigmoid(SWIGLU_ALPHA * g) * (up + 1.0)
        return act @ w2[e] + b2[e]

    yb = lax.map(expert_block, (xb, blk_e)).reshape(n_slots, d)
    return jax.ops.segment_sum(yb * slot_gate[:, None], slot_tok, num_segments=n_tok + 1)[:n_tok]


def attend_prompt(qa, ka, va, lam, qb, kb, vb, logf):
    return diff_attn_prompt(qa, ka, va, lam), fox_prompt(qb, kb, vb, logf)


def attend_sample(qa, ka, va, lam, qb, kb, vb, logf, ka_past, va_past, kb_past, vb_past, logf_past):
    return (diff_attn_sample(qa, ka, va, lam, ka_past, va_past),
            fox_sample(qb, kb, vb, logf, kb_past, vb_past, logf_past))


def decoder_layer(h, p_l, pos, attend, layer, w):
    (g_mix, w_in, b_f, gq_a, gk_a, lam_q1, lam_k1, lam_q2, lam_k2, g_subln, gq_b, gk_b, g_sgu,
     w_s, b_s, w_out, g_moe, w_router, b_router, w1, b1, w2, b2, g_ple, w_ple_gate, w_ple) = w
    n1 = rms_norm(h, g_mix)
    qa, ka, va, qb, kb, vb, logf, uc, vc = project(n1, pos, w_in, b_f, gq_a, gk_a, gq_b, gk_b, g_sgu)
    lam_init = 0.8 - 0.6 * math.exp(-0.3 * layer)
    f32 = jnp.float32
    lam = (jnp.exp(jnp.sum(lam_q1.astype(f32) * lam_k1.astype(f32)))
           - jnp.exp(jnp.sum(lam_q2.astype(f32) * lam_k2.astype(f32))) + lam_init)
    oa, ob = attend(qa, ka, va, lam, qb, kb, vb, logf)
    oa = rms_norm(oa, g_subln) * (1.0 - lam_init)
    oc = chunk_mlp(uc, vc, w_s, b_s)
    bsz, t = h.shape[:2]
    mixed = jnp.concatenate([oa.reshape(bsz, t, -1), ob.reshape(bsz, t, -1), oc.reshape(bsz, t, -1)], axis=-1)
    h = h + mixed @ w_out
    h = h + moe(rms_norm(h, g_moe).reshape(bsz * t, -1), w_router, b_router, w1, b1, w2, b2).reshape(h.shape)
    gate = jax.nn.sigmoid(rms_norm(h, g_ple) @ w_ple_gate)
    h = h + gate * (p_l @ w_ple)
    return h, (ka, va, kb, vb, logf, vc)


def setup_inputs(seed: int = 0) -> dict:
    key = jax.random.key(seed)
    ks = iter(jax.random.split(key, 48))
    f32 = jnp.float32

    def nrm(shape, scale):
        return jax.random.normal(next(ks), shape, f32) * scale

    def gain(shape):
        return 1.0 + 0.02 * jax.random.normal(next(ks), shape, f32)

    n_pages = PAST_LEN // PAGE_SIZE
    n_used = DEC_BATCH * n_pages
    n_pool = n_used + n_used // 4
    page_table = jax.random.permutation(next(ks), n_pool)[:n_used].reshape(DEC_BATCH, n_pages).astype(jnp.int32)
    return {
        'x_prompt': nrm((BATCH, SEQ, D_MODEL), 1.0),
        'x_sample': nrm((DEC_BATCH, DEC_SEQ, D_MODEL), 1.0),
        'cache_a_k': nrm((DEPTH, n_pool, PAGE_SIZE, HA, 2, DK_A), 1.0),
        'cache_a_v': nrm((DEPTH, n_pool, PAGE_SIZE, HA, DV_A), 1.0),
        'cache_b_k': nrm((DEPTH, n_pool, PAGE_SIZE, HB, DH_B), 1.0),
        'cache_b_v': nrm((DEPTH, n_pool, PAGE_SIZE, HB, DH_B), 1.0),
        'cache_b_logf': jax.nn.log_sigmoid(3.0 + nrm((DEPTH, n_pool, PAGE_SIZE, HB), 1.0)),
        'page_table': page_table,
        'p_prompt': nrm((DEPTH, BATCH, SEQ, D_PLE), 1.0),
        'p_sample': nrm((DEPTH, DEC_BATCH, DEC_SEQ, D_PLE), 1.0),
        'g_mix': gain((DEPTH, D_MODEL)),
        'w_in': nrm((DEPTH, D_MODEL, N_IN), D_MODEL ** -0.5),
        'b_f': 3.0 + nrm((DEPTH, HB), 0.1),
        'gq_a': gain((DEPTH, DK_A)),
        'gk_a': gain((DEPTH, DK_A)),
        'lam_q1': nrm((DEPTH, DK_A), 0.1),
        'lam_k1': nrm((DEPTH, DK_A), 0.1),
        'lam_q2': nrm((DEPTH, DK_A), 0.1),
        'lam_k2': nrm((DEPTH, DK_A), 0.1),
        'g_subln': gain((DEPTH, DV_A)),
        'gq_b': gain((DEPTH, DH_B)),
        'gk_b': gain((DEPTH, DH_B)),
        'g_sgu': gain((DEPTH, DH_C)),
        'w_s': nrm((DEPTH, HC, CHUNK, CHUNK), CHUNK ** -0.5),
        'b_s': gain((DEPTH, HC, CHUNK)),
        'w_out': nrm((DEPTH, D_MIX, D_MODEL), D_MIX ** -0.5),
        'g_moe': gain((DEPTH, D_MODEL)),
        'w_router': nrm((DEPTH, D_MODEL, N_EXPERTS), D_MODEL ** -0.5),
        'b_router': nrm((DEPTH, N_EXPERTS), 0.01),
        'w1': nrm((DEPTH, N_EXPERTS, D_MODEL, 2 * D_FF), D_MODEL ** -0.5),
        'b1': nrm((DEPTH, N_EXPERTS, 2 * D_FF), 0.01),
        'w2': nrm((DEPTH, N_EXPERTS, D_FF, D_MODEL), D_FF ** -0.5),
        'b2': nrm((DEPTH, N_EXPERTS, D_MODEL), 0.01),
        'g_ple': gain((DEPTH, D_MODEL)),
        'w_ple_gate': nrm((DEPTH, D_MODEL, D_MODEL), D_MODEL ** -0.5),
        'w_ple': nrm((DEPTH, D_PLE, D_MODEL), D_PLE ** -0.5),
    }


def reference(x_prompt, x_sample, cache_a_k, cache_a_v, cache_b_k, cache_b_v, cache_b_logf, page_table,
              p_prompt, p_sample, g_mix, w_in, b_f, gq_a, gk_a, lam_q1, lam_k1, lam_q2, lam_k2, g_subln,
              gq_b, gk_b, g_sgu, w_s, b_s, w_out, g_moe, w_router, b_router, w1, b1, w2, b2, g_ple,
              w_ple_gate, w_ple):
    dec_b = page_table.shape[0]
    past_len = page_table.shape[1] * cache_a_k.shape[2]
    pos_prompt = jnp.arange(x_prompt.shape[1], dtype=jnp.int32)
    pos_sample = past_len + jnp.arange(x_sample.shape[1], dtype=jnp.int32)
    hp, hs = x_prompt, x_sample
    rows_p, rows_s = [], []
    for l in range(DEPTH):
        w = (g_mix[l], w_in[l], b_f[l], gq_a[l], gk_a[l], lam_q1[l], lam_k1[l], lam_q2[l], lam_k2[l],
             g_subln[l], gq_b[l], gk_b[l], g_sgu[l], w_s[l], b_s[l], w_out[l], g_moe[l], w_router[l],
             b_router[l], w1[l], b1[l], w2[l], b2[l], g_ple[l], w_ple_gate[l], w_ple[l])
        hp, st_p = decoder_layer(hp, p_prompt[l], pos_prompt, attend_prompt, l, w)
        rows_p.append(st_p)
        attend_fn = functools.partial(
            attend_sample,
            ka_past=cache_a_k[l][page_table].reshape(dec_b, past_len, HA, 2, DK_A),
            va_past=cache_a_v[l][page_table].reshape(dec_b, past_len, HA, DV_A),
            kb_past=cache_b_k[l][page_table].reshape(dec_b, past_len, HB, DH_B),
            vb_past=cache_b_v[l][page_table].reshape(dec_b, past_len, HB, DH_B),
            logf_past=cache_b_logf[l][page_table].reshape(dec_b, past_len, HB))
        hs, st_s = decoder_layer(hs, p_sample[l], pos_sample, attend_fn, l, w)
        rows_s.append(st_s)
    new_a_k_prompt = jnp.stack([r[0] for r in rows_p])
    new_a_v_prompt = jnp.stack([r[1] for r in rows_p])
    new_b_k_prompt = jnp.stack([r[2] for r in rows_p])
    new_b_v_prompt = jnp.stack([r[3] for r in rows_p])
    new_b_logf_prompt = jnp.stack([r[4] for r in rows_p])
    new_a_k_sample = jnp.stack([r[0] for r in rows_s])
    new_a_v_sample = jnp.stack([r[1] for r in rows_s])
    new_b_k_sample = jnp.stack([r[2] for r in rows_s])
    new_b_v_sample = jnp.stack([r[3] for r in rows_s])
    new_b_logf_sample = jnp.stack([r[4] for r in rows_s])
    new_c_v_sample = jnp.stack([r[5] for r in rows_s])
    return (hp, hs, new_a_k_prompt, new_a_v_prompt, new_b_k_prompt, new_b_v_prompt, new_b_logf_prompt,
            new_a_k_sample, new_a_v_sample, new_b_k_sample, new_b_v_sample, new_b_logf_sample, new_c_v_sample)
```

```python
import functools
import math

import numpy as np
import jax
import jax.numpy as jnp
from jax import lax
from jax.experimental import pallas as pl
from jax.experimental.pallas import tpu as pltpu

F32 = jnp.float32
BF16 = jnp.bfloat16
I32 = jnp.int32

HA, DK_A, DV_A = 4, 64, 128
HB, DH_B = 4, 64
HC, DH_C = 4, 64
CHUNK = 128
ROPE_THETA = 500000.0
ROT_DIM = DK_A // 4
N_EXPERTS = 32
TOP_K = 4
SWIGLU_LIMIT = 7.0
SWIGLU_ALPHA = 1.702
EPS = 1e-6

LANES = 128
WA = HA * 2 * DK_A
WB = HB * DH_B
WC = HC * DH_C
O_QA, O_KA, O_VA = 0, WA, 2 * WA
O_QB = 3 * WA
O_KB, O_VB = O_QB + WB, O_QB + 2 * WB
O_UC, O_VC = O_QB + 3 * WB, O_QB + 4 * WB
O_F = O_QB + 5 * WB
N_COLS = O_F + LANES

VMEM_LIMIT = 56 * 1024 * 1024
NEG_INF = float("-inf")


def _cparams(sem):
    return pltpu.CompilerParams(dimension_semantics=sem, vmem_limit_bytes=VMEM_LIMIT)


def _full(shape):
    nd = len(shape)
    return pl.BlockSpec(shape, lambda *_: (0,) * nd)


def _rms(x, g):
    return x * lax.rsqrt(jnp.mean(x * x, axis=-1, keepdims=True) + EPS) * g


def _split3(x):
    hi = x.astype(BF16)
    r = x - hi.astype(F32)
    mid = r.astype(BF16)
    lo = (r - mid.astype(F32)).astype(BF16)
    return hi, mid, lo


def _dot(a, b):
    return jnp.dot(a, b, preferred_element_type=F32)


def _dot_nt(a, b):
    return lax.dot_general(a, b, (((1,), (1,)), ((), ())), preferred_element_type=F32)


def _split2(x):
    hi = x.astype(BF16)
    return hi, (x - hi.astype(F32)).astype(BF16)


def _group_rms(z, gmat, gain, width):
    zz = z * z
    hi = zz.astype(BF16)
    lo = (zz - hi.astype(F32)).astype(BF16)
    ss = _dot(hi, gmat) + _dot(lo, gmat)
    return z * lax.rsqrt(ss * (1.0 / width) + EPS) * gain


def _group_rms_exact(z, gmat, gain, width):
    del gmat
    zz = z * z
    lane = lax.broadcasted_iota(I32, z.shape, 1)
    ss = jnp.zeros_like(z)
    for g in range(z.shape[1] // width):
        m = (lane >= g * width) & (lane < (g + 1) * width)
        ss = jnp.where(m, jnp.sum(jnp.where(m, zz, 0.0), axis=-1, keepdims=True), ss)
    return z * lax.rsqrt(ss * (1.0 / width) + EPS) * gain


def _rope(y, ct, s1, s2):
    reps = y.shape[1] // LANES
    c = jnp.concatenate([ct] * reps, axis=1)
    a1 = jnp.concatenate([s1] * reps, axis=1)
    a2 = jnp.concatenate([s2] * reps, axis=1)
    half = ROT_DIM // 2
    up = pltpu.roll(y, y.shape[1] - half, axis=1)
    dn = pltpu.roll(y, half, axis=1)
    return y * c + up * a1 + dn * a2


def _log_sigmoid(x):
    return jnp.minimum(x, 0.0) - jnp.log1p(jnp.exp(-jnp.abs(x)))


def _inproj_body(h_ref, gmix_ref, w_ref, bf_ref, gqa_ref, gka_ref, gqb_ref, gkb_ref, gsgu_ref,
                 g512_ref, g256_ref, ct_ref, s1_ref, s2_ref, grms=_group_rms):
    n1 = _rms(h_ref[...], gmix_ref[...]).astype(BF16)

    def sec(off, width):
        return _dot(n1, w_ref[:, off:off + width])

    ct, s1, s2 = ct_ref[...], s1_ref[...], s2_ref[...]
    g512, g256 = g512_ref[...], g256_ref[...]
    qa = _rope(grms(sec(O_QA, WA), g512, gqa_ref[...], DK_A), ct, s1, s2)
    ka = _rope(grms(sec(O_KA, WA), g512, gka_ref[...], DK_A), ct, s1, s2)
    va = sec(O_VA, WA)
    qb = grms(sec(O_QB, WB), g256, gqb_ref[...], DH_B)
    kb = grms(sec(O_KB, WB), g256, gkb_ref[...], DH_B)
    vb = sec(O_VB, WB)
    uc = sec(O_UC, WC)
    vc = grms(sec(O_VC, WC), g256, gsgu_ref[...], DH_C)
    lane = lax.broadcasted_iota(I32, (n1.shape[0], LANES), 1)
    logf = jnp.where(lane < HB, _log_sigmoid(sec(O_F, LANES) + bf_ref[...]), 0.0)
    return qa, ka, va, qb, kb, vb, uc, vc, logf


def _inproj_prompt_kernel(h_ref, gmix_ref, w_ref, bf_ref, gqa_ref, gka_ref, gqb_ref, gkb_ref, gsgu_ref,
                          g512_ref, g256_ref, ct_ref, s1_ref, s2_ref, ws_ref, bst_ref,
                          qa_ref, ka_ref, va_ref, qb_ref, kb_ref, vb_ref, logf_ref, oc_ref, c_ref, ct_out_ref,
                          carry_ref, *, tm, tiles_per_seq):
    qa, ka, va, qb, kb, vb, uc, vc, logf = _inproj_body(
        h_ref, gmix_ref, w_ref, bf_ref, gqa_ref, gka_ref, gqb_ref, gkb_ref, gsgu_ref,
        g512_ref, g256_ref, ct_ref, s1_ref, s2_ref)
    qa_ref[...] = (qa * DK_A ** -0.5).astype(BF16)
    ka_ref[...] = ka
    va_ref[...] = va
    qb_ref[...] = (qb * DH_B ** -0.5).astype(BF16)
    kb_ref[...] = kb
    vb_ref[...] = vb
    logf_ref[...] = logf

    r = lax.broadcasted_iota(I32, (CHUNK, CHUNK), 0)
    cidx = lax.broadcasted_iota(I32, (CHUNK, CHUNK), 1)
    lane = lax.broadcasted_iota(I32, (CHUNK, WC), 1)
    wts = [jnp.where(cidx <= r, ws_ref[g], 0.0).astype(BF16) for g in range(HC)]
    vcb = vc.astype(BF16)
    for c in range(tm // CHUNK):
        rows = slice(c * CHUNK, (c + 1) * CHUNK)
        sg = bst_ref[...]
        for g in range(HC):
            full = _dot(wts[g], vcb[rows])
            sg = sg + jnp.where((lane >= g * DH_C) & (lane < (g + 1) * DH_C), full, 0.0)
        oc_ref[rows, :] = (uc[rows] * sg).astype(BF16)

    i = pl.program_id(0)

    @pl.when(i % tiles_per_seq == 0)
    def _():
        carry_ref[...] = jnp.zeros_like(carry_ref)

    rr = lax.broadcasted_iota(I32, (tm, tm), 0)
    cc = lax.broadcasted_iota(I32, (tm, tm), 1)
    tri = jnp.where(cc <= rr, 1.0, 0.0).astype(BF16)
    hi, mid, lo = _split3(logf)
    csum = _dot(tri, hi) + _dot(tri, mid) + _dot(tri, lo) + carry_ref[...]
    c_ref[...] = csum
    carry_ref[...] = csum[tm - 1:tm, :]
    ct_out_ref[...] = csum.T[:8, :]


def _inproj_sample_kernel(h_ref, gmix_ref, w_ref, bf_ref, gqa_ref, gka_ref, gqb_ref, gkb_ref, gsgu_ref,
                          g512_ref, g256_ref, ct_ref, s1_ref, s2_ref, w00_ref, b0_ref,
                          qa_ref, ka_ref, va_ref, qb_ref, kb_ref, vb_ref, logf_ref, oc_ref, vc_ref):
    qa, ka, va, qb, kb, vb, uc, vc, logf = _inproj_body(
        h_ref, gmix_ref, w_ref, bf_ref, gqa_ref, gka_ref, gqb_ref, gkb_ref, gsgu_ref,
        g512_ref, g256_ref, ct_ref, s1_ref, s2_ref, grms=_group_rms_exact)
    qa_ref[...] = qa * DK_A ** -0.5
    ka_ref[...] = ka
    va_ref[...] = va
    qb_ref[...] = qb * DH_B ** -0.5
    kb_ref[...] = kb
    vb_ref[...] = vb
    logf_ref[...] = logf
    vc_ref[...] = vc
    oc_ref[...] = uc * (vc * w00_ref[...] + b0_ref[...])


def _inproj_common_args(wts, tabs, tm, tab_index):
    d = wts["w_cat"].shape[0]
    specs = [
        pl.BlockSpec((tm, d), lambda i: (i, 0)),
        _full((1, d)), _full((d, N_COLS)), _full((1, LANES)),
        _full((1, WA)), _full((1, WA)), _full((1, WB)), _full((1, WB)), _full((1, WC)),
        _full((WA, WA)), _full((WB, WB)),
        pl.BlockSpec((tm, LANES), tab_index), pl.BlockSpec((tm, LANES), tab_index),
        pl.BlockSpec((tm, LANES), tab_index),
    ]
    args = [wts["g_mix"], wts["w_cat"], wts["bf_row"], wts["gqa"], wts["gka"], wts["gqb"], wts["gkb"],
            wts["gsgu"], wts["g512"], wts["g256"], tabs[0], tabs[1], tabs[2]]
    return specs, args


def _inproj_prompt(h, wts, tabs, seq, tm):
    rows, _ = h.shape
    tiles_per_seq = seq // tm
    specs, args = _inproj_common_args(wts, tabs, tm, lambda i: (i % tiles_per_seq, 0))
    specs += [_full((HC, CHUNK, CHUNK)), _full((CHUNK, WC))]
    args += [wts["w_s"], wts["bs_t"]]
    row = lambda w: pl.BlockSpec((tm, w), lambda i: (i, 0))
    out_shape = [
        jax.ShapeDtypeStruct((rows, WA), BF16), jax.ShapeDtypeStruct((rows, WA), F32),
        jax.ShapeDtypeStruct((rows, WA), F32), jax.ShapeDtypeStruct((rows, WB), BF16),
        jax.ShapeDtypeStruct((rows, WB), F32), jax.ShapeDtypeStruct((rows, WB), F32),
        jax.ShapeDtypeStruct((rows, LANES), F32), jax.ShapeDtypeStruct((rows, WC), BF16),
        jax.ShapeDtypeStruct((rows, LANES), F32),
        jax.ShapeDtypeStruct((rows // seq, tiles_per_seq, 8, tm), F32),
    ]
    out_specs = [row(WA), row(WA), row(WA), row(WB), row(WB), row(WB), row(LANES), row(WC), row(LANES),
                 pl.BlockSpec((None, None, 8, tm), lambda i: (i // tiles_per_seq, i % tiles_per_seq, 0, 0))]
    return pl.pallas_call(
        functools.partial(_inproj_prompt_kernel, tm=tm, tiles_per_seq=tiles_per_seq),
        grid=(rows // tm,), in_specs=specs, out_specs=out_specs, out_shape=out_shape,
        scratch_shapes=[pltpu.VMEM((1, LANES), F32)],
        compiler_params=_cparams(("arbitrary",)), name="inproj_prompt",
    )(h, *args)


def _inproj_sample(h, wts, tabs):
    rows, _ = h.shape
    specs, args = _inproj_common_args(wts, tabs, rows, lambda i: (i, 0))
    specs += [_full((1, WC)), _full((1, WC))]
    args += [wts["w00"], wts["b0"]]
    row = lambda w: pl.BlockSpec((rows, w), lambda i: (i, 0))
    out_shape = [jax.ShapeDtypeStruct((rows, w), F32) for w in (WA, WA, WA, WB, WB, WB, LANES, WC, WC)]
    out_specs = [row(WA), row(WA), row(WA), row(WB), row(WB), row(WB), row(LANES), row(WC), row(WC)]
    return pl.pallas_call(
        _inproj_sample_kernel, grid=(1,), in_specs=specs, out_specs=out_specs, out_shape=out_shape,
        compiler_params=_cparams(("arbitrary",)), name="inproj_sample",
    )(h, *args)


def _lam_value(q1_ref, k1_ref, q2_ref, k2_ref, lam_init):
    return (jnp.exp(jnp.sum(q1_ref[...] * k1_ref[...], axis=-1, keepdims=True))
            - jnp.exp(jnp.sum(q2_ref[...] * k2_ref[...], axis=-1, keepdims=True)) + lam_init)


def _online_update(s, v, m_ref, l_ref, acc_ref, idx):
    m_old = m_ref[idx]
    m_new = jnp.maximum(m_old, jnp.max(s, axis=-1, keepdims=True))
    alpha = jnp.exp(m_old - m_new)
    p = jnp.exp(s - m_new)
    l_ref[idx] = alpha * l_ref[idx] + jnp.sum(p, axis=-1, keepdims=True)
    acc_ref[idx] = alpha * acc_ref[idx] + _dot(p.astype(BF16), v)
    m_ref[idx] = m_new


def _attn_a_kernel(q1_ref, k1_ref, q2_ref, k2_ref, gsub_ref, q_ref, k_ref, v_ref, o_ref,
                   kbf_ref, vbf_ref, m_ref, l_ref, acc_ref, *, tq, lam_init):
    i = pl.program_id(2)

    @pl.when(i == 0)
    def _():
        kbf_ref[...] = k_ref[...].astype(BF16)
        vbf_ref[...] = v_ref[...].astype(BF16)

    q = q_ref[...]
    lane = lax.broadcasted_iota(I32, q.shape, 1)
    zero = jnp.zeros_like(q)
    qs = (jnp.where(lane < DK_A, q, zero), jnp.where(lane >= DK_A, q, zero))
    m_ref[...] = jnp.full(m_ref.shape, NEG_INF, F32)
    l_ref[...] = jnp.zeros(l_ref.shape, F32)
    acc_ref[...] = jnp.zeros(acc_ref.shape, F32)
    row = lax.broadcasted_iota(I32, (tq, tq), 0)
    col = lax.broadcasted_iota(I32, (tq, tq), 1)

    def tile(j, diagonal):
        start = pl.multiple_of(j * tq, tq)
        kj = kbf_ref[pl.ds(start, tq), :]
        vj = vbf_ref[pl.ds(start, tq), :]
        for c in range(2):
            s = _dot_nt(qs[c], kj)
            if diagonal:
                s = jnp.where(col <= row, s, NEG_INF)
            _online_update(s, vj, m_ref, l_ref, acc_ref, c)

    def body(j, carry):
        tile(j, False)
        return carry

    lax.fori_loop(0, i, body, 0)
    tile(i, True)

    lam = _lam_value(q1_ref, k1_ref, q2_ref, k2_ref, lam_init)
    o = acc_ref[0] / l_ref[0] - lam * (acc_ref[1] / l_ref[1])
    o_ref[...] = (_rms(o, gsub_ref[...]) * (1.0 - lam_init)).astype(BF16)


def _attn_a_prompt(qa, ka, va, wts, batch, seq, tq, lam_init):
    nq = seq // tq
    small = _full((1, DK_A))
    return pl.pallas_call(
        functools.partial(_attn_a_kernel, tq=tq, lam_init=lam_init),
        grid=(batch, HA, nq),
        in_specs=[small, small, small, small, _full((1, DV_A)),
                  pl.BlockSpec((tq, DV_A), lambda b, h, i: (b * nq + i, h)),
                  pl.BlockSpec((seq, DV_A), lambda b, h, i: (b, h)),
                  pl.BlockSpec((seq, DV_A), lambda b, h, i: (b, h))],
        out_specs=pl.BlockSpec((tq, DV_A), lambda b, h, i: (b * nq + i, h)),
        out_shape=jax.ShapeDtypeStruct(qa.shape, BF16),
        scratch_shapes=[pltpu.VMEM((seq, DV_A), BF16), pltpu.VMEM((seq, DV_A), BF16),
                        pltpu.VMEM((2, tq, 1), F32), pltpu.VMEM((2, tq, 1), F32),
                        pltpu.VMEM((2, tq, DV_A), F32)],
        compiler_params=_cparams(("arbitrary", "arbitrary", "arbitrary")), name="attn_a_prompt",
    )(wts["lam_q1"], wts["lam_k1"], wts["lam_q2"], wts["lam_k2"], wts["g_subln"], qa, ka, va)


def _attn_b_kernel(q_ref, k_ref, v_ref, c_ref, ct_ref, o_ref, kbf_ref, vbf_ref, m_ref, l_ref, acc_ref, *, tq):
    i = pl.program_id(1)

    @pl.when(i == 0)
    def _():
        kbf_ref[...] = k_ref[...].astype(BF16)
        vbf_ref[...] = v_ref[...].astype(BF16)

    q = q_ref[...]
    lane = lax.broadcasted_iota(I32, q.shape, 1)
    zero = jnp.zeros_like(q)
    qs = [jnp.where((lane >= h * DH_B) & (lane < (h + 1) * DH_B), q, zero) for h in range(HB)]
    cq = c_ref[...]
    m_ref[...] = jnp.full(m_ref.shape, NEG_INF, F32)
    l_ref[...] = jnp.zeros(l_ref.shape, F32)
    acc_ref[...] = jnp.zeros(acc_ref.shape, F32)
    row = lax.broadcasted_iota(I32, (tq, tq), 0)
    col = lax.broadcasted_iota(I32, (tq, tq), 1)

    def tile(j, diagonal):
        start = pl.multiple_of(j * tq, tq)
        kj = kbf_ref[pl.ds(start, tq), :]
        vj = vbf_ref[pl.ds(start, tq), :]
        ck = ct_ref[j]
        for h in range(HB):
            s = _dot_nt(qs[h], kj) + cq[:, h:h + 1] - ck[h:h + 1, :]
            if diagonal:
                s = jnp.where(col <= row, s, NEG_INF)
            _online_update(s, vj, m_ref, l_ref, acc_ref, h)

    def body(j, carry):
        tile(j, False)
        return carry

    lax.fori_loop(0, i, body, 0)
    tile(i, True)

    olane = lax.broadcasted_iota(I32, (tq, WB), 1)
    o = jnp.zeros((tq, WB), F32)
    for h in range(HB):
        o = o + jnp.where((olane >= h * DH_B) & (olane < (h + 1) * DH_B), acc_ref[h] / l_ref[h], 0.0)
    o_ref[...] = o.astype(BF16)


def _attn_b_prompt(qb, kb, vb, c, ct, batch, seq, tq):
    nq = seq // tq
    return pl.pallas_call(
        functools.partial(_attn_b_kernel, tq=tq),
        grid=(batch, nq),
        in_specs=[pl.BlockSpec((tq, WB), lambda b, i: (b * nq + i, 0)),
                  pl.BlockSpec((seq, WB), lambda b, i: (b, 0)),
                  pl.BlockSpec((seq, WB), lambda b, i: (b, 0)),
                  pl.BlockSpec((tq, LANES), lambda b, i: (b * nq + i, 0)),
                  pl.BlockSpec((None, nq, 8, tq), lambda b, i: (b, 0, 0, 0))],
        out_specs=pl.BlockSpec((tq, WB), lambda b, i: (b * nq + i, 0)),
        out_shape=jax.ShapeDtypeStruct(qb.shape, BF16),
        scratch_shapes=[pltpu.VMEM((seq, WB), BF16), pltpu.VMEM((seq, WB), BF16),
                        pltpu.VMEM((HB, tq, 1), F32), pltpu.VMEM((HB, tq, 1), F32),
                        pltpu.VMEM((HB, tq, WB), F32)],
        compiler_params=_cparams(("arbitrary", "arbitrary")), name="attn_b_prompt",
    )(qb, kb, vb, c, ct)


def _suffix_matrix(page):
    m = np.zeros((page * HB, 8 * page), np.float32)
    pos = np.arange(page)
    for h in range(HB):
        src = pos * HB + h
        m[np.ix_(src, h * page + pos)] = (pos[:, None] > pos[None, :]).astype(np.float32)
        m[np.ix_(src, (HB + h) * page + pos)] = 1.0
    return m


def _suffix_kernel(x_ref, m_ref, o_ref):
    hi, mid, lo = _split3(x_ref[...])
    m = m_ref[...]
    o_ref[...] = _dot(hi, m) + _dot(mid, m) + _dot(lo, m)


def _logf_suffix(cache_b_logf, tp):
    depth, n_pool, page, hb = cache_b_logf.shape
    x = cache_b_logf.reshape(depth * n_pool, page * hb)
    rows = x.shape[0]
    mat = jnp.asarray(_suffix_matrix(page), BF16)
    out = pl.pallas_call(
        _suffix_kernel, grid=(rows // tp,),
        in_specs=[pl.BlockSpec((tp, page * hb), lambda i: (i, 0)), _full(mat.shape)],
        out_specs=pl.BlockSpec((tp, 8 * page), lambda i: (i, 0)),
        out_shape=jax.ShapeDtypeStruct((rows, 8 * page), F32),
        compiler_params=_cparams(("arbitrary",)), name="logf_suffix",
    )(x, mat)
    return out.reshape(depth, n_pool, 8, page)


def _query_rows(qa_ref, qb_ref):
    ra = lax.broadcasted_iota(I32, (8, WA), 0)
    la = lax.broadcasted_iota(I32, (8, WA), 1)
    qa_bd = jnp.where(la // DK_A == (ra % HA) * 2 + ra // HA, qa_ref[...], 0.0)
    rb = lax.broadcasted_iota(I32, (8, WB), 0)
    lb = lax.broadcasted_iota(I32, (8, WB), 1)
    qb_bd = jnp.where(lb // DH_B == rb, qb_ref[...], 0.0)
    return qa_bd, qb_bd


def _sample_scores_kernel(pt_ref, qa_ref, qb_ref, lfn_ref, *rest, gp):
    ka_refs, kb_refs, sf_refs = rest[0:gp], rest[gp:2 * gp], rest[2 * gp:3 * gp]
    sa_ref, sb_ref, carry_ref = rest[3 * gp:]
    qa_bd, qb_bd = (q.astype(BF16) for q in _query_rows(qa_ref, qb_ref))

    @pl.when(pl.program_id(1) == 0)
    def _():
        rr = lax.broadcasted_iota(I32, (8, LANES), 0)
        ll = lax.broadcasted_iota(I32, (8, LANES), 1)
        pick = jnp.where(ll == rr, lfn_ref[...], 0.0)
        carry_ref[...] = jnp.broadcast_to(jnp.sum(pick, axis=-1, keepdims=True), (8, LANES))

    carry = carry_ref[...]
    sa, sb = [], []
    for g in range(gp):
        sa.append(_dot_nt(qa_bd, ka_refs[g][...].astype(BF16)))
        sf = sf_refs[g][...]
        sb.append(_dot_nt(qb_bd, kb_refs[g][...].astype(BF16)) + (sf + carry))
        carry = carry + pltpu.roll(sf, 4, axis=0)
    carry_ref[...] = carry
    sa_ref[...] = jnp.concatenate(sa, axis=1)
    sb_ref[...] = jnp.concatenate(sb, axis=1)


def _sample_probs_kernel(q1_ref, k1_ref, q2_ref, k2_ref, qa_ref, kan_ref, qb_ref, kbn_ref, sa_ref, sb_ref,
                         pa_ref, pb_ref, pn_ref, *, lam_init):
    qa_bd, qb_bd = _query_rows(qa_ref, qb_ref)
    lam = _lam_value(q1_ref, k1_ref, q2_ref, k2_ref, lam_init)

    def softmax(s, s_new):
        m = jnp.maximum(jnp.max(s, axis=-1, keepdims=True), s_new)
        e = jnp.exp(s - m)
        e_new = jnp.exp(s_new - m)
        tot = jnp.sum(e, axis=-1, keepdims=True) + e_new
        return e / tot, e_new / tot

    sna = jnp.sum(qa_bd * kan_ref[...], axis=-1, keepdims=True)
    snb = jnp.sum(qb_bd * kbn_ref[...], axis=-1, keepdims=True)
    pra, pna = softmax(sa_ref[...], sna)
    prb, pnb = softmax(sb_ref[...], snb)
    row = lax.broadcasted_iota(I32, pra.shape, 0)
    pa_ref[...] = jnp.where(row < HA, pra - lam * pltpu.roll(pra, HA, axis=0), 0.0)
    pb16 = jnp.where(row < HB, prb, 0.0).astype(BF16)
    pb_ref[...] = jnp.concatenate([pb16, jnp.zeros_like(pb16)], axis=0)
    rown = lax.broadcasted_iota(I32, (8, 1), 0)
    pdn = jnp.where(rown < HA, pna - lam * pltpu.roll(jnp.broadcast_to(pna, (8, LANES)), HA, axis=0)[:, :1], 0.0)
    pnb = jnp.where(rown < HB, pnb, 0.0)
    pn_ref[...] = jnp.broadcast_to(jnp.concatenate([pdn, pnb], axis=0), (16, LANES))


def _sample_values_kernel(pt_ref, gsub_ref, pa_ref, pb_ref, pn_ref, van_ref, vbn_ref, *rest,
                          gp, n_chunks, lam_init):
    va_refs, vb_refs = rest[0:gp], rest[gp:2 * gp]
    oa_ref, ob_ref, acca_ref, accb_ref = rest[2 * gp:]
    cc = pl.program_id(1)
    page = va_refs[0].shape[0]

    @pl.when(cc == 0)
    def _():
        acca_ref[...] = jnp.zeros(acca_ref.shape, F32)
        accb_ref[...] = jnp.zeros(accb_ref.shape, F32)

    pa_all = pa_ref[...]
    pb_all = pb_ref[...]
    acc_a = jnp.zeros((8, WA), F32)
    acc_b = jnp.zeros((16, WB), F32)
    for g in range(gp):
        cols = slice(g * page, (g + 1) * page)
        ph, plo = _split2(pa_all[:, cols])
        vh, vl = _split2(va_refs[g][...])
        both = _dot(jnp.concatenate([ph, plo], axis=0), vh)
        acc_a = acc_a + (both[:8] + (both[8:] + _dot(ph, vl)))
        acc_b = acc_b + _dot(pb_all[:, cols], vb_refs[g][...].astype(BF16))
    acca_ref[...] += acc_a
    accb_ref[...] += acc_b

    @pl.when(cc == n_chunks - 1)
    def _():
        pn = pn_ref[...][:, :1]
        na = acca_ref[...] + pn[:8] * van_ref[...]
        outs = []
        for h in range(HA):
            o = na[h:h + 1, h * DV_A:(h + 1) * DV_A]
            outs.append(_rms(o, gsub_ref[...]) * (1.0 - lam_init))
        oa_ref[...] = jnp.concatenate(outs, axis=1)
        nb = accb_ref[...][:8] + pn[8:] * vbn_ref[...]
        rb = lax.broadcasted_iota(I32, (8, WB), 0)
        lb = lax.broadcasted_iota(I32, (8, WB), 1)
        ob_ref[...] = jnp.sum(jnp.where(lb // DH_B == rb, nb, 0.0), axis=0, keepdims=True)


def _attn_sample(layer, page_table, caches, suffix, new, wts, gp, lam_init):
    cache_a_k, cache_a_v, cache_b_k, cache_b_v = caches
    qa, ka_n, va_n, qb, kb_n, vb_n, lf_n = new
    nb, n_pages = page_table.shape
    page = cache_a_k.shape[2]
    n_chunks = n_pages // gp
    past = n_pages * page
    r3 = lambda x: x.reshape(nb, 1, x.shape[-1])
    small = _full((1, DK_A))
    per_b = lambda w: pl.BlockSpec((None, 1, w), lambda b, c, pt: (b, 0, 0))

    def page_spec(rows, width, g):
        return pl.BlockSpec((None, None, rows, width),
                            lambda b, c, pt: (layer, pt[b, n_pages - 1 - (c * gp + g)], 0, 0))

    chunk = lambda rows: pl.BlockSpec((None, rows, gp * page), lambda b, c, pt: (b, 0, c))

    in_specs = [per_b(WA), per_b(WB), per_b(LANES)]
    in_specs += [page_spec(page, WA, g) for g in range(gp)] + [page_spec(page, WB, g) for g in range(gp)]
    in_specs += [page_spec(8, page, g) for g in range(gp)]
    sa, sb = pl.pallas_call(
        functools.partial(_sample_scores_kernel, gp=gp),
        grid_spec=pltpu.PrefetchScalarGridSpec(
            num_scalar_prefetch=1, grid=(nb, n_chunks), in_specs=in_specs,
            out_specs=[chunk(8), chunk(8)], scratch_shapes=[pltpu.VMEM((8, LANES), F32)]),
        out_shape=[jax.ShapeDtypeStruct((nb, 8, past), F32), jax.ShapeDtypeStruct((nb, 8, past), F32)],
        compiler_params=_cparams(("arbitrary", "arbitrary")), name="sample_scores",
    )(page_table, r3(qa), r3(qb), r3(lf_n),
      *([cache_a_k] * gp), *([cache_b_k] * gp), *([suffix] * gp))

    vec = lambda w: pl.BlockSpec((None, 1, w), lambda b: (b, 0, 0))
    whole = lambda rows, w: pl.BlockSpec((None, rows, w), lambda b: (b, 0, 0))
    pa, pb, pn = pl.pallas_call(
        functools.partial(_sample_probs_kernel, lam_init=lam_init), grid=(nb,),
        in_specs=[small, small, small, small, vec(WA), vec(WA), vec(WB), vec(WB), whole(8, past), whole(8, past)],
        out_specs=[whole(8, past), whole(16, past), whole(16, LANES)],
        out_shape=[jax.ShapeDtypeStruct((nb, 8, past), F32), jax.ShapeDtypeStruct((nb, 16, past), BF16),
                   jax.ShapeDtypeStruct((nb, 16, LANES), F32)],
        compiler_params=_cparams(("arbitrary",)), name="sample_probs",
    )(wts["lam_q1"], wts["lam_k1"], wts["lam_q2"], wts["lam_k2"], r3(qa), r3(ka_n), r3(qb), r3(kb_n), sa, sb)

    in_specs = [_full((1, DV_A)), chunk(8), chunk(16),
                pl.BlockSpec((None, 16, LANES), lambda b, c, pt: (b, 0, 0)), per_b(WA), per_b(WB)]
    in_specs += [page_spec(page, WA, g) for g in range(gp)] + [page_spec(page, WB, g) for g in range(gp)]
    oa, ob = pl.pallas_call(
        functools.partial(_sample_values_kernel, gp=gp, n_chunks=n_chunks, lam_init=lam_init),
        grid_spec=pltpu.PrefetchScalarGridSpec(
            num_scalar_prefetch=1, grid=(nb, n_chunks), in_specs=in_specs,
            out_specs=[per_b(WA), per_b(WB)],
            scratch_shapes=[pltpu.VMEM((8, WA), F32), pltpu.VMEM((16, WB), F32)]),
        out_shape=[jax.ShapeDtypeStruct((nb, 1, WA), F32), jax.ShapeDtypeStruct((nb, 1, WB), F32)],
        compiler_params=_cparams(("arbitrary", "arbitrary")), name="sample_values",
    )(page_table, wts["g_subln"], pa, pb, pn, r3(va_n), r3(vb_n), *([cache_a_v] * gp), *([cache_b_v] * gp))
    return oa.reshape(nb, WA), ob.reshape(nb, WB)


def _outproj_kernel(h_ref, oa_ref, ob_ref, oc_ref, wo_ref, gmoe_ref, wr_ref, br_ref,
                    h1_ref, xn_ref, te_ref, gt_ref):
    delta = (_dot(oa_ref[...].astype(BF16), wo_ref[0:WA, :])
             + _dot(ob_ref[...].astype(BF16), wo_ref[WA:WA + WB, :])
             + _dot(oc_ref[...].astype(BF16), wo_ref[WA + WB:WA + WB + WC, :]))
    h1 = h_ref[...] + delta
    h1_ref[...] = h1
    xn = _rms(h1, gmoe_ref[...])
    xn_ref[...] = xn
    logits = _dot(xn.astype(BF16), wr_ref[...]) + br_ref[...]
    tm = logits.shape[0]
    lane = lax.broadcasted_iota(I32, (tm, LANES), 1)
    work = logits
    te = jnp.zeros((tm, LANES), I32)
    tv = jnp.zeros((tm, LANES), F32)
    for k in range(TOP_K):
        mx = jnp.max(work, axis=-1, keepdims=True)
        idx = jnp.min(jnp.where(work == mx, lane, LANES), axis=-1, keepdims=True)
        te = jnp.where(lane == k, idx, te)
        tv = jnp.where(lane == k, mx, tv)
        work = jnp.where(lane == idx, NEG_INF, work)
    top = jnp.max(jnp.where(lane == 0, tv, NEG_INF), axis=-1, keepdims=True)
    ex = jnp.where(lane < TOP_K, jnp.exp(tv - top), 0.0)
    gt_ref[...] = ex / jnp.sum(ex, axis=-1, keepdims=True)
    te_ref[...] = te


def _outproj(h, oa, ob, oc, wts, tm):
    rows, d = h.shape
    row = lambda w: pl.BlockSpec((tm, w), lambda i: (i, 0))
    return pl.pallas_call(
        _outproj_kernel, grid=(rows // tm,),
        in_specs=[row(d), row(WA), row(WB), row(WC), _full((WA + WB + WC, d)), _full((1, d)),
                  _full((d, LANES)), _full((1, LANES))],
        out_specs=[row(d), row(d), row(LANES), row(LANES)],
        out_shape=[jax.ShapeDtypeStruct((rows, d), F32), jax.ShapeDtypeStruct((rows, d), F32),
                   jax.ShapeDtypeStruct((rows, LANES), I32), jax.ShapeDtypeStruct((rows, LANES), F32)],
        compiler_params=_cparams(("arbitrary",)), name="outproj_router",
    )(h, oa, ob, oc, wts["w_out"], wts["g_moe"], wts["w_router"], wts["br_row"])


def _moe_kernel(be_ref, nu_ref, src_ref, nxt_ref, dst_ref, xn_hbm, w1_ref, b1_ref, w2_ref, b2_ref, y_hbm,
                xbuf, ybuf, w1b, w2b, gsem, ssem, *, tm, dff, n_tok, plane):
    i = pl.program_id(0)
    n_used = nu_ref[0]
    prev = be_ref[jnp.maximum(i - 1, 0)]
    changed = (i == 0) | (be_ref[i] != prev)
    slot = i % 2

    def gather(idx_ref, to_slot):
        def body(r, carry):
            pltpu.make_async_copy(xn_hbm.at[pl.ds(idx_ref[0, r], 1), :],
                                  xbuf.at[to_slot, pl.ds(r, 1), :], gsem.at[to_slot]).start()
            return carry
        lax.fori_loop(0, tm, body, 0, unroll=8)

    def wait_rows(src, dst, sem):
        def body(r, carry):
            pltpu.make_async_copy(src, dst, sem).wait()
            return carry
        lax.fori_loop(0, tm, body, 0, unroll=8)

    row0 = pl.ds(0, 1)

    @pl.when(i == 0)
    def _():
        ybuf[...] = jnp.zeros(ybuf.shape, F32)
        fills = [(TOP_K * plane, tm)]
        for k in range(TOP_K):
            for off in range(n_tok, plane, tm):
                fills.append((k * plane + off, min(tm, plane - off)))
        copies = [pltpu.make_async_copy(ybuf.at[pl.ds(0, n), :], y_hbm.at[pl.ds(r0, n), :], ssem)
                  for r0, n in fills]
        for cp in copies:
            cp.start()
        for cp in copies:
            cp.wait()
        gather(src_ref, 0)

    @pl.when(i < n_used)
    def _():
        @pl.when(i + 1 < n_used)
        def _():
            gather(nxt_ref, 1 - slot)

        @pl.when(changed)
        def _():
            w1b[...] = w1_ref[...].astype(BF16)
            w2b[...] = w2_ref[...].astype(BF16)

        wait_rows(xn_hbm.at[row0, :], xbuf.at[slot, row0, :], gsem.at[slot])
        x = xbuf[slot].astype(BF16)
        hmid = _dot(x, w1b[...]) + b1_ref[...]
        g = jnp.minimum(hmid[:, :dff], SWIGLU_LIMIT)
        up = jnp.clip(hmid[:, dff:], -SWIGLU_LIMIT, SWIGLU_LIMIT)
        act = (g * jax.nn.sigmoid(SWIGLU_ALPHA * g) * (up + 1.0)).astype(BF16)

        @pl.when(i > 0)
        def _():
            wait_rows(ybuf.at[row0, :], y_hbm.at[row0, :], ssem)

        ybuf[...] = _dot(act, w2b[...]) + b2_ref[...]

        def scatter(r, carry):
            pltpu.make_async_copy(ybuf.at[pl.ds(r, 1), :], y_hbm.at[pl.ds(dst_ref[0, r], 1), :], ssem).start()
            return carry

        lax.fori_loop(0, tm, scatter, 0, unroll=8)

        @pl.when(i == n_used - 1)
        def _():
            wait_rows(ybuf.at[row0, :], y_hbm.at[row0, :], ssem)


def _moe(layer, xn, blk_e, n_used, slot_src, slot_dst, w1, b1, w2, b2, tm, plane):
    n_tok, d = xn.shape
    n_blocks = blk_e.shape[0]
    dff = w2.shape[2]
    depth, ne = w1.shape[:2]
    src3 = slot_src.reshape(n_blocks, 1, tm)
    smem = lambda index: pl.BlockSpec((None, 1, tm), index, memory_space=pltpu.SMEM)
    grid_spec = pltpu.PrefetchScalarGridSpec(
        num_scalar_prefetch=2, grid=(n_blocks,),
        in_specs=[
            smem(lambda i, be, nu: (i, 0, 0)),
            smem(lambda i, be, nu: (jnp.minimum(i + 1, n_blocks - 1), 0, 0)),
            smem(lambda i, be, nu: (i, 0, 0)),
            pl.BlockSpec(memory_space=pl.ANY),
            pl.BlockSpec((None, None, d, 2 * dff), lambda i, be, nu: (layer, be[i], 0, 0)),
            pl.BlockSpec((None, None, 1, 2 * dff), lambda i, be, nu: (layer, be[i], 0, 0)),
            pl.BlockSpec((None, None, dff, d), lambda i, be, nu: (layer, be[i], 0, 0)),
            pl.BlockSpec((None, None, 1, d), lambda i, be, nu: (layer, be[i], 0, 0)),
        ],
        out_specs=pl.BlockSpec(memory_space=pl.ANY),
        scratch_shapes=[pltpu.VMEM((2, tm, d), F32), pltpu.VMEM((tm, d), F32),
                        pltpu.VMEM((d, 2 * dff), BF16), pltpu.VMEM((dff, d), BF16),
                        pltpu.SemaphoreType.DMA((2,)), pltpu.SemaphoreType.DMA])
    return pl.pallas_call(
        functools.partial(_moe_kernel, tm=tm, dff=dff, n_tok=n_tok, plane=plane), grid_spec=grid_spec,
        out_shape=jax.ShapeDtypeStruct((TOP_K * plane + tm, d), F32),
        compiler_params=_cparams(("arbitrary",)), name="moe_experts",
    )(blk_e, n_used, src3, src3, slot_dst.reshape(n_blocks, 1, tm), xn,
      w1, b1.reshape(depth, ne, 1, 2 * dff), w2, b2.reshape(depth, ne, 1, d))


def _route(top_e, tm, plane):
    n_tok = top_e.shape[0]
    n_as = n_tok * TOP_K
    flat_e = top_e.reshape(n_as)
    order = jnp.argsort(flat_e).astype(I32)
    se = flat_e[order]
    counts = jnp.bincount(flat_e, length=N_EXPERTS).astype(I32)
    padded = (counts + tm - 1) // tm * tm
    pend = jnp.cumsum(padded)
    pstart = pend - padded
    start = jnp.cumsum(counts) - counts
    dest = pstart[se] + jnp.arange(n_as, dtype=I32) - start[se]
    n_blocks = -(-n_as // tm) + N_EXPERTS
    n_slots = n_blocks * tm
    tok = order // TOP_K
    kk = order % TOP_K
    slot_src = jnp.zeros((n_slots,), I32).at[dest].set(tok)
    dump = TOP_K * plane + jnp.arange(n_slots, dtype=I32) % tm
    slot_dst = dump.at[dest].set(kk * plane + tok)
    n_used = (pend[-1] // tm).astype(I32)
    blk = jnp.minimum(jnp.searchsorted(pend, jnp.arange(n_blocks, dtype=I32) * tm, side="right"),
                      N_EXPERTS - 1).astype(I32)
    blk = jnp.where(jnp.arange(n_blocks) < n_used, blk, blk[jnp.maximum(n_used - 1, 0)])
    return blk, n_used.reshape(1), slot_src, slot_dst


def _ple_kernel(h1_ref, y0_ref, y1_ref, y2_ref, y3_ref, gt_ref, p_ref, gple_ref, wg_ref, wp_ref, o_ref):
    gt = gt_ref[...]
    h2 = h1_ref[...]
    for k, y_ref in enumerate((y0_ref, y1_ref, y2_ref, y3_ref)):
        h2 = h2 + gt[:, k:k + 1] * y_ref[...]
    n = _rms(h2, gple_ref[...]).astype(BF16)
    gate = jax.nn.sigmoid(_dot(n, wg_ref[...]))
    o_ref[...] = h2 + gate * _dot(p_ref[...].astype(BF16), wp_ref[...])


def _ple(h1, y, gates, p, wts, tm, row_off, plane):
    rows, d = h1.shape
    dp = p.shape[1]
    row = lambda w: pl.BlockSpec((tm, w), lambda i: (i, 0))
    yspec = lambda k: pl.BlockSpec((tm, d), lambda i: ((k * plane + row_off) // tm + i, 0))
    return pl.pallas_call(
        _ple_kernel, grid=(rows // tm,),
        in_specs=[row(d)] + [yspec(k) for k in range(TOP_K)] + [row(LANES), row(dp),
                  _full((1, d)), _full((d, d)), _full((dp, d))],
        out_specs=row(d), out_shape=jax.ShapeDtypeStruct((rows, d), F32),
        compiler_params=_cparams(("arbitrary",)), name="combine_ple",
    )(h1, y, y, y, y, gates, p, wts["g_ple"], wts["w_ple_gate"], wts["w_ple"])


def _rope_tables(pos):
    half = ROT_DIM // 2
    inv = ROPE_THETA ** (-2.0 * jnp.arange(half, dtype=F32) / ROT_DIM)
    ang = pos.astype(F32)[:, None] * inv[None, :]
    cos, sin = jnp.cos(ang), jnp.sin(ang)
    n = pos.shape[0]
    rest = DK_A - ROT_DIM
    ct = jnp.concatenate([cos, cos, jnp.ones((n, rest), F32)], axis=1)
    s1 = jnp.concatenate([-sin, jnp.zeros((n, half + rest), F32)], axis=1)
    s2 = jnp.concatenate([jnp.zeros((n, half), F32), sin, jnp.zeros((n, rest), F32)], axis=1)
    rep = LANES // DK_A
    return tuple(jnp.tile(t, (1, rep)) for t in (ct, s1, s2))


def _layer_weights(l, g_mix, w_in, b_f, gq_a, gk_a, lam_q1, lam_k1, lam_q2, lam_k2, g_subln, gq_b, gk_b,
                   g_sgu, w_s, b_s, w_out, g_moe, w_router, b_router, g_ple, w_ple_gate, w_ple):
    d = w_in.shape[1]
    wi = w_in[l]
    f0 = O_QB + 3 * WB
    w_cat = jnp.concatenate([wi[:, :f0], wi[:, f0 + HB:], wi[:, f0:f0 + HB],
                             jnp.zeros((d, LANES - HB), F32)], axis=1).astype(BF16)
    gidx = lambda w, grp: (np.arange(w)[:, None] // grp == np.arange(w)[None, :] // grp)
    return dict(
        g_mix=g_mix[l][None], w_cat=w_cat, bf_row=jnp.pad(b_f[l], (0, LANES - HB))[None],
        gqa=jnp.tile(gq_a[l], WA // DK_A)[None], gka=jnp.tile(gk_a[l], WA // DK_A)[None],
        gqb=jnp.tile(gq_b[l], HB)[None], gkb=jnp.tile(gk_b[l], HB)[None], gsgu=jnp.tile(g_sgu[l], HC)[None],
        g512=jnp.asarray(gidx(WA, DK_A), BF16), g256=jnp.asarray(gidx(WB, DH_B), BF16),
        w_s=w_s[l], bs_t=jnp.repeat(b_s[l].T, DH_C, axis=1),
        w00=jnp.repeat(w_s[l][:, 0, 0], DH_C)[None], b0=jnp.repeat(b_s[l][:, 0], DH_C)[None],
        lam_q1=lam_q1[l][None], lam_k1=lam_k1[l][None], lam_q2=lam_q2[l][None], lam_k2=lam_k2[l][None],
        g_subln=g_subln[l][None], w_out=w_out[l].astype(BF16), g_moe=g_moe[l][None],
        w_router=jnp.pad(w_router[l], ((0, 0), (0, LANES - N_EXPERTS))).astype(BF16),
        br_row=jnp.concatenate([b_router[l], jnp.full((LANES - N_EXPERTS,), NEG_INF, F32)])[None],
        g_ple=g_ple[l][None], w_ple_gate=w_ple_gate[l].astype(BF16), w_ple=w_ple[l].astype(BF16),
    )


def _tile(n, pref):
    t = min(pref, n)
    while n % t:
        t //= 2
    return t


def kernel(x_prompt, x_sample, cache_a_k, cache_a_v, cache_b_k, cache_b_v, cache_b_logf, page_table, p_prompt, p_sample, g_mix, w_in, b_f, gq_a, gk_a, lam_q1, lam_k1, lam_q2, lam_k2, g_subln, gq_b, gk_b, g_sgu, w_s, b_s, w_out, g_moe, w_router, b_router, w1, b1, w2, b2, g_ple, w_ple_gate, w_ple):
    batch, seq, d = x_prompt.shape
    nb, dec_seq, _ = x_sample.shape
    assert dec_seq == 1, "the sample stream carries one new token per sequence"
    depth, n_pool, page = cache_a_k.shape[:3]
    n_pages = page_table.shape[1]
    past_len = n_pages * page
    rp = batch * seq
    n_tok = rp + nb

    tq = _tile(seq, 256)
    tm_dense = _tile(rp, 512)
    tm_moe = 256
    gp = _tile(n_pages, 8)
    assert rp % nb == 0 and tm_dense % nb == 0
    plane = -(-n_tok // tm_dense) * tm_dense

    tabs_p = _rope_tables(jnp.arange(seq, dtype=I32))
    tabs_s = _rope_tables(jnp.full((nb,), past_len, I32))
    ca_k = cache_a_k.reshape(depth, n_pool, page, WA)
    ca_v = cache_a_v.reshape(depth, n_pool, page, WA)
    cb_k = cache_b_k.reshape(depth, n_pool, page, WB)
    cb_v = cache_b_v.reshape(depth, n_pool, page, WB)
    suffix = _logf_suffix(cache_b_logf, _tile(depth * n_pool, 512))

    hp = x_prompt.reshape(rp, d)
    hs = x_sample.reshape(nb, d)
    rows_p, rows_s = [], []
    for l in range(depth):
        lam_init = 0.8 - 0.6 * math.exp(-0.3 * l)
        wts = _layer_weights(l, g_mix, w_in, b_f, gq_a, gk_a, lam_q1, lam_k1, lam_q2, lam_k2, g_subln, gq_b,
                             gk_b, g_sgu, w_s, b_s, w_out, g_moe, w_router, b_router, g_ple, w_ple_gate, w_ple)
        qa, ka, va, qb, kb, vb, logf, oc, c, ct = _inproj_prompt(hp, wts, tabs_p, seq, tq)
        oa = _attn_a_prompt(qa, ka, va, wts, batch, seq, tq, lam_init)
        ob = _attn_b_prompt(qb, kb, vb, c, ct, batch, seq, tq)
        h1p, xnp_, tep, gtp = _outproj(hp, oa, ob, oc, wts, tm_dense)
        rows_p.append((ka, va, kb, vb, logf[:, :HB]))
        qa_s, ka_s, va_s, qb_s, kb_s, vb_s, lf_s, oc_s, vc_s = _inproj_sample(hs, wts, tabs_s)
        oa_s, ob_s = _attn_sample(l, page_table, (ca_k, ca_v, cb_k, cb_v), suffix,
                                  (qa_s, ka_s, va_s, qb_s, kb_s, vb_s, lf_s), wts, gp, lam_init)
        h1s, xns, tes, gts = _outproj(hs, oa_s, ob_s, oc_s, wts, nb)
        rows_s.append((ka_s, va_s, kb_s, vb_s, lf_s[:, :HB], vc_s))
        xn = jnp.concatenate([xnp_, xns], axis=0)
        top_e = jnp.concatenate([tep[:, :TOP_K], tes[:, :TOP_K]], axis=0)
        blk_e, n_used, slot_src, slot_dst = _route(top_e, tm_moe, plane)
        y = _moe(l, xn, blk_e, n_used, slot_src, slot_dst, w1, b1, w2, b2, tm_moe, plane)
        hp = _ple(h1p, y, gtp, p_prompt[l].reshape(rp, -1), wts, tm_dense, 0, plane)
        hs = _ple(h1s, y, gts, p_sample[l].reshape(nb, -1), wts, nb, rp, plane)

    def stack(rows, idx, shape):
        return jnp.stack([r[idx] for r in rows]).reshape((depth,) + shape)

    return (hp.reshape(batch, seq, d), hs.reshape(nb, 1, d),
            stack(rows_p, 0, (batch, seq, HA, 2, DK_A)), stack(rows_p, 1, (batch, seq, HA, DV_A)),
            stack(rows_p, 2, (batch, seq, HB, DH_B)), stack(rows_p, 3, (batch, seq, HB, DH_B)),
            stack(rows_p, 4, (batch, seq, HB)),
            stack(rows_s, 0, (nb, 1, HA, 2, DK_A)), stack(rows_s, 1, (nb, 1, HA, DV_A)),
            stack(rows_s, 2, (nb, 1, HB, DH_B)), stack(rows_s, 3, (nb, 1, HB, DH_B)),
            stack(rows_s, 4, (nb, 1, HB)), stack(rows_s, 5, (nb, 1, HC, DH_C)))
```
